```python
import math
import jax
import jax.numpy as jnp
from jax import lax
import numpy as np

D_MODEL = 1024
BATCH = 8
SEQ = 4096
DEPTH = 1
DEC_BATCH = 128
DEC_SEQ = 8
PAST_LEN = 8192
PAGE_SIZE = 128

H_A = 8
DH_A = 64
D_A = H_A * DH_A
W_RANK = 64
A_RANK = 64
D_SHIFT = 3 * D_A + W_RANK + A_RANK
GN_EPS = 64e-5
H_B = 4
DH_B = 64
D_B = H_B * 2 * DH_B
ROPE_THETA = 10000.0
Q_BLOCK = 128
IN_SPLITS = (D_SHIFT, D_A, D_B, D_B, D_B, D_B, D_MODEL, D_MODEL)
D_IN = D_SHIFT + D_A + 4 * D_B + 2 * D_MODEL

kernel_name = 'rwkv7_diffattn_gated_hybrid_step'


def _rms_norm(x, g, eps=1e-6):
    xf = x.astype(jnp.float32)
    y = xf * lax.rsqrt(jnp.mean(xf * xf, axis=-1, keepdims=True) + eps)
    return (y * g.astype(jnp.float32)).astype(x.dtype)


def _rope(x, pos):
    half = DH_B // 2
    inv = ROPE_THETA ** (-jnp.arange(half, dtype=jnp.float32) / half)
    ang = pos.astype(jnp.float32)[:, None] * inv[None, :]
    cos = jnp.cos(ang)[:, None, None, :]
    sin = jnp.sin(ang)[:, None, None, :]
    xf = x.astype(jnp.float32)
    x1, x2 = xf[..., :half], xf[..., half:]
    return jnp.concatenate([x1 * cos - x2 * sin, x2 * cos + x1 * sin], axis=-1).astype(x.dtype)


def _rwkv7(p_shift, prev_row, s0, mu, w0, w_up, a0, a_up, k_k, k_a, r_k, gn_w, gn_b):
    B, T, _ = p_shift.shape
    shifted = jnp.concatenate([prev_row.astype(p_shift.dtype)[:, None, :], p_shift[:, :-1]], axis=1)
    xm = p_shift + (shifted - p_shift) * mu
    r, k, v, wd, ad = jnp.split(xm, [D_A, 2 * D_A, 3 * D_A, 3 * D_A + W_RANK], axis=-1)
    w = (w0 + jnp.tanh(wd) @ w_up).astype(jnp.float32)
    log_w = -jax.nn.softplus(-w) - 0.5
    decay = jnp.exp(-jnp.exp(log_w))
    a = jax.nn.sigmoid((a0 + ad @ a_up).astype(jnp.float32))

    def heads(z):
        return z.reshape(B, T, H_A, DH_A).astype(jnp.float32)

    r, k, v, decay, a = heads(r), heads(k), heads(v), heads(decay), heads(a)
    kk = k * k_k.reshape(H_A, DH_A).astype(jnp.float32)
    kk = kk / jnp.maximum(jnp.sqrt(jnp.sum(kk * kk, axis=-1, keepdims=True)), 1e-12)
    k = k * (1.0 + (a - 1.0) * k_a.reshape(H_A, DH_A).astype(jnp.float32))

    def step(S, inp):
        r_t, w_t, k_t, v_t, kk_t, a_t = inp
        sa = jnp.einsum('bhvk,bhk->bhv', S, -kk_t)
        S = (S * w_t[:, :, None, :] + sa[..., None] * (kk_t * a_t)[:, :, None, :]
             + v_t[..., None] * k_t[:, :, None, :])
        return S, jnp.einsum('bhvk,bhk->bhv', S, r_t)

    seq_major = lambda z: jnp.swapaxes(z, 0, 1)
    s_final, o = lax.scan(step, s0.astype(jnp.float32),
                          (seq_major(r), seq_major(decay), seq_major(k), seq_major(v), seq_major(kk), seq_major(a)))
    o = seq_major(o)
    mean = jnp.mean(o, axis=-1, keepdims=True)
    var = jnp.mean(jnp.square(o - mean), axis=-1, keepdims=True)
    o = ((o - mean) * lax.rsqrt(var + GN_EPS) * gn_w.reshape(H_A, DH_A).astype(jnp.float32)
         + gn_b.reshape(H_A, DH_A).astype(jnp.float32))
    o = o + jnp.sum(r * k * r_k.astype(jnp.float32), axis=-1, keepdims=True) * v
    return o.reshape(B, T, D_A).astype(p_shift.dtype), s_final, p_shift[:, -1]


def _diff_core(q, k, v, mask, lam):
    s = jnp.einsum('bqhcd,bkhcd->bhcqk', q, k, preferred_element_type=jnp.float32) * (DH_B ** -0.5)
    s = jnp.where(mask, s, -jnp.inf)
    p = jax.nn.softmax(s, axis=-1)
    p_diff = p[:, :, 0] - lam * p[:, :, 1]
    return jnp.einsum('bhqk,bkhe->bqhe', p_diff.astype(v.dtype), v)


def _prompt_attn(q, k, v, lam):
    B, T = q.shape[:2]
    nb = T // Q_BLOCK
    qb = jnp.moveaxis(q.reshape(B, nb, Q_BLOCK, H_B, 2, DH_B), 1, 0)
    kpos = jnp.arange(T)

    def block(args):
        qi, i = args
        qpos = i * Q_BLOCK + jnp.arange(Q_BLOCK)
        return _diff_core(qi, k, v, kpos[None, :] <= qpos[:, None], lam)

    o = lax.map(block, (qb, jnp.arange(nb)))
    return jnp.moveaxis(o, 0, 1).reshape(B, T, H_B, 2 * DH_B)


def _sample_attn(q, k_new, v_new, lam, cache_k, cache_v, page_table):
    past = page_table.shape[1] * PAGE_SIZE
    t_new = q.shape[1]
    qpos = past + jnp.arange(t_new)
    kpos = jnp.arange(past + t_new)
    mask = kpos[None, :] <= qpos[:, None]

    def one(args):
        pt, qi, ki, vi = args
        kp = cache_k[pt].reshape(past, H_B, 2, DH_B).astype(ki.dtype)
        vp = cache_v[pt].reshape(past, H_B, 2 * DH_B).astype(vi.dtype)
        k_all = jnp.concatenate([kp, ki], axis=0)[None]
        v_all = jnp.concatenate([vp, vi], axis=0)[None]
        return _diff_core(qi[None], k_all, v_all, mask, lam)[0]

    return lax.map(one, (page_table, q, k_new, v_new))


def _layer(x, pos, s0, prev_row, attend, lam_init, norm_w, w_in, mu_shift, w0, w_up, a0, a_up,
           k_k, k_a, r_k, gn_w, gn_b, q_norm_w, k_norm_w, lambda_q1, lambda_k1, lambda_q2,
           lambda_k2, subln_w, w_branch_a, w_branch_b, w_out):
    B, T, _ = x.shape
    h = _rms_norm(x, norm_w)
    p = jnp.einsum('btd,de->bte', h, w_in)
    offs = np.cumsum(IN_SPLITS)[:-1].tolist()
    p_shift, g_a, q, k, v, g_b, m_a, m_b = jnp.split(p, offs, axis=-1)
    o_a, s_final, shift_row = _rwkv7(p_shift, prev_row, s0, mu_shift, w0, w_up, a0, a_up,
                                     k_k, k_a, r_k, gn_w, gn_b)
    q = _rope(_rms_norm(q.reshape(B, T, H_B, 2, DH_B), q_norm_w), pos)
    k = _rope(_rms_norm(k.reshape(B, T, H_B, 2, DH_B), k_norm_w), pos)
    v = v.reshape(B, T, H_B, 2 * DH_B)
    f32 = jnp.float32
    lam = (jnp.exp(jnp.sum(lambda_q1.astype(f32) * lambda_k1.astype(f32)))
           - jnp.exp(jnp.sum(lambda_q2.astype(f32) * lambda_k2.astype(f32))) + lam_init)
    o_b = attend(q, k, v, lam)
    o_b = (_rms_norm(o_b, subln_w, 1e-5) * (1.0 - lam_init)).reshape(B, T, D_B)
    y_a = (o_a * jax.nn.silu(g_a)) @ w_branch_a
    y_b = (o_b * jax.nn.silu(g_b)) @ w_branch_b
    merged = jax.nn.sigmoid(m_a) * y_a + jax.nn.sigmoid(m_b) * y_b
    y = x + merged @ w_out
    return y, k, v, s_final, shift_row


def setup_inputs(seed: int = 0) -> dict:
    key = jax.random.key(seed)
    ks = jax.random.split(key, 32)
    L = DEPTH
    n_pages = PAST_LEN // PAGE_SIZE
    n_used = DEC_BATCH * n_pages
    n_pool = (5 * n_used) // 4
    nrm = jax.random.normal
    perm = jax.random.permutation(ks[0], n_pool)
    page_table = perm[:n_used].reshape(DEC_BATCH, n_pages).astype(jnp.int32)
    return {
        'x_prompt': nrm(ks[1], (BATCH, SEQ, D_MODEL), jnp.float32),
        'x_sample': nrm(ks[2], (DEC_BATCH, DEC_SEQ, D_MODEL), jnp.float32),
        'cache_k': nrm(ks[3], (L, n_pool, PAGE_SIZE, H_B, 2, DH_B), jnp.float32),
        'cache_v': nrm(ks[4], (L, n_pool, PAGE_SIZE, H_B, 2 * DH_B), jnp.float32),
        'state_wkv': 0.2 * nrm(ks[5], (L, DEC_BATCH, H_A, DH_A, DH_A), jnp.float32),
        'state_shift': nrm(ks[6], (L, DEC_BATCH, D_SHIFT), jnp.float32),
        'page_table': page_table,
        'norm_w': 1.0 + 0.05 * nrm(ks[7], (L, D_MODEL), jnp.float32),
        'w_in': nrm(ks[8], (L, D_MODEL, D_IN), jnp.float32) * D_MODEL ** -0.5,
        'mu_shift': jax.random.uniform(ks[9], (L, D_SHIFT), jnp.float32),
        'w0': -3.0 + 0.5 * nrm(ks[10], (L, D_A), jnp.float32),
        'w_up': 0.1 * nrm(ks[11], (L, W_RANK, D_A), jnp.float32),
        'a0': 0.1 * nrm(ks[12], (L, D_A), jnp.float32),
        'a_up': 0.1 * nrm(ks[13], (L, A_RANK, D_A), jnp.float32),
        'k_k': 0.85 + 0.05 * nrm(ks[14], (L, D_A), jnp.float32),
        'k_a': 1.0 + 0.05 * nrm(ks[15], (L, D_A), jnp.float32),
        'r_k': 0.1 * nrm(ks[16], (L, H_A, DH_A), jnp.float32),
        'gn_w': 1.0 + 0.05 * nrm(ks[17], (L, D_A), jnp.float32),
        'gn_b': 0.01 * nrm(ks[18], (L, D_A), jnp.float32),
        'q_norm_w': 1.0 + 0.05 * nrm(ks[19], (L, 2, DH_B), jnp.float32),
        'k_norm_w': 1.0 + 0.05 * nrm(ks[20], (L, 2, DH_B), jnp.float32),
        'lambda_q1': 0.1 * nrm(ks[21], (L, DH_B), jnp.float32),
        'lambda_k1': 0.1 * nrm(ks[22], (L, DH_B), jnp.float32),
        'lambda_q2': 0.1 * nrm(ks[23], (L, DH_B), jnp.float32),
        'lambda_k2': 0.1 * nrm(ks[24], (L, DH_B), jnp.float32),
        'subln_w': 1.0 + 0.05 * nrm(ks[25], (L, 2 * DH_B), jnp.float32),
        'w_branch_a': nrm(ks[26], (L, D_A, D_MODEL), jnp.float32) * D_A ** -0.5,
        'w_branch_b': nrm(ks[27], (L, D_B, D_MODEL), jnp.float32) * D_B ** -0.5,
        'w_out': nrm(ks[28], (L, D_MODEL, D_MODEL), jnp.float32) * D_MODEL ** -0.5,
    }


def reference(x_prompt, x_sample, cache_k, cache_v, state_wkv, state_shift, page_table,
              norm_w, w_in, mu_shift, w0, w_up, a0, a_up, k_k, k_a, r_k, gn_w, gn_b,
              q_norm_w, k_norm_w, lambda_q1, lambda_k1, lambda_q2, lambda_k2, subln_w,
              w_branch_a, w_branch_b, w_out):
    b_p, t_p = x_prompt.shape[0], x_prompt.shape[1]
    t_s = x_sample.shape[1]
    past = page_table.shape[1] * PAGE_SIZE
    pos_p = jnp.arange(t_p, dtype=jnp.int32)
    pos_s = past + jnp.arange(t_s, dtype=jnp.int32)
    y_p, y_s = x_prompt, x_sample
    kp, vp, sp, hp, ksm, vsm, ssm, hsm = [], [], [], [], [], [], [], []
    for l in range(DEPTH):
        lam_init = 0.8 - 0.6 * math.exp(-0.3 * l)
        lw = (norm_w[l], w_in[l], mu_shift[l], w0[l], w_up[l], a0[l], a_up[l], k_k[l], k_a[l],
              r_k[l], gn_w[l], gn_b[l], q_norm_w[l], k_norm_w[l], lambda_q1[l], lambda_k1[l],
              lambda_q2[l], lambda_k2[l], subln_w[l], w_branch_a[l], w_branch_b[l], w_out[l])
        zero_state = jnp.zeros((b_p, H_A, DH_A, DH_A), jnp.float32)
        zero_row = jnp.zeros((b_p, D_SHIFT), x_prompt.dtype)
        y_p, k_r, v_r, s_r, h_r = _layer(y_p, pos_p, zero_state, zero_row, _prompt_attn, lam_init, *lw)
        kp.append(k_r)
        vp.append(v_r)
        sp.append(s_r.astype(state_wkv.dtype))
        hp.append(h_r.astype(state_shift.dtype))
        ck, cv = cache_k[l], cache_v[l]
        attend_s = lambda q, k, v, lam, ck=ck, cv=cv: _sample_attn(q, k, v, lam, ck, cv, page_table)
        y_s, k_r, v_r, s_r, h_r = _layer(y_s, pos_s, state_wkv[l], state_shift[l], attend_s, lam_init, *lw)
        ksm.append(k_r.astype(cache_k.dtype))
        vsm.append(v_r.astype(cache_v.dtype))
        ssm.append(s_r.astype(state_wkv.dtype))
        hsm.append(h_r.astype(state_shift.dtype))
    return (y_p, y_s, jnp.stack(kp), jnp.stack(vp), jnp.stack(sp), jnp.stack(hp),
            jnp.stack(ksm), jnp.stack(vsm), jnp.stack(ssm), jnp.stack(hsm))
```

```python
import functools
import math

import jax
import jax.numpy as jnp
from jax import lax
from jax.experimental import pallas as pl
from jax.experimental.pallas import tpu as pltpu

F32 = jnp.float32
BF16 = jnp.bfloat16

H_A, DH_A = 8, 64
D_A = H_A * DH_A
W_RANK = A_RANK = 64
D_SHIFT = 3 * D_A + W_RANK + A_RANK
GN_EPS = 64e-5
H_B, DH_B = 4, 64
D_B = H_B * 2 * DH_B
ROPE_THETA = 10000.0
HEADS_PER_GROUP = 4
GROUP_W = HEADS_PER_GROUP * DH_A
N_GROUPS = H_A // HEADS_PER_GROUP
LANES = 128
VMEM_LIMIT = 56 * 1024 * 1024


def _dot(a, b):
    return jnp.dot(a, b, preferred_element_type=F32)


def _dot_nt(a, b):
    return lax.dot_general(a, b, (((1,), (1,)), ((), ())), preferred_element_type=F32)


def _dot_tn(a, b):
    return lax.dot_general(a, b, (((0,), (0,)), ((), ())), preferred_element_type=F32)


def _split2(x):
    hi = x.astype(BF16)
    lo = (x - hi.astype(F32)).astype(BF16)
    return hi, lo


def _split3(x):
    hi = x.astype(BF16)
    r = x - hi.astype(F32)
    mid = r.astype(BF16)
    lo = (r - mid.astype(F32)).astype(BF16)
    return hi, mid, lo


def _dot_sel2(x, sel):
    hi, lo = _split2(x)
    return _dot(hi, sel) + _dot(lo, sel)


def _dot_sel3(x, sel):
    hi, mid, lo = _split3(x)
    return _dot(hi, sel) + _dot(mid, sel) + _dot(lo, sel)


def _sel3_dot(sel, x):
    hi, mid, lo = _split3(x)
    return _dot(sel, hi) + _dot(sel, mid) + _dot(sel, lo)


def _sigmoid(x):
    return 1.0 / (1.0 + jnp.exp(-x))


def _group_sum(x, ones_bd):
    parts = [_dot_sel2(x[:, i:i + GROUP_W], ones_bd) for i in range(0, x.shape[1], GROUP_W)]
    return parts[0] if len(parts) == 1 else jnp.concatenate(parts, axis=1)


def _inproj_kernel(x_ref, nw_ref, w_ref, cos_ref, sin_ref, qg_ref, kg_ref, ones_ref,
                   ps_ref, ga_ref, q_ref, kf_ref, kb_ref, vf_ref, vb_ref, gb_ref, ma_ref, mb_ref):
    x = x_ref[...]
    ms = jnp.mean(x * x, axis=-1, keepdims=True)
    h = (x * lax.rsqrt(ms + 1e-6) * nw_ref[...]).astype(BF16)

    def proj(lo, width):
        return _dot(h, w_ref[:, lo:lo + width])

    o = 0
    ps_ref[...] = proj(o, D_SHIFT)
    o += D_SHIFT
    g = proj(o, D_A)
    ga_ref[...] = (g * _sigmoid(g)).astype(BF16)
    o += D_A

    cos = cos_ref[...]
    sin = sin_ref[...]
    lane = lax.broadcasted_iota(jnp.int32, (1, LANES), 1)
    first_half = (lane % DH_B) < (DH_B // 2)
    ones_bd = ones_ref[...]

    def norm_rope(z, g_ref):
        msq = _group_sum(z * z, ones_bd) * (1.0 / DH_B)
        y = z * lax.rsqrt(msq + 1e-6) * g_ref[...]
        outs = []
        for c in range(0, D_B, LANES):
            yc = y[:, c:c + LANES]
            fwd = pltpu.roll(yc, DH_B // 2, axis=1)
            bwd = pltpu.roll(yc, LANES - DH_B // 2, axis=1)
            outs.append(yc * cos + jnp.where(first_half, bwd, fwd) * sin)
        return jnp.concatenate(outs, axis=1)

    q = norm_rope(proj(o, D_B), qg_ref)
    q_ref[...] = (q * (DH_B ** -0.5)).astype(BF16)
    o += D_B
    k = norm_rope(proj(o, D_B), kg_ref)
    kf_ref[...] = k
    kb_ref[...] = k.astype(BF16)
    o += D_B
    v = proj(o, D_B)
    vf_ref[...] = v
    vb_ref[...] = v.astype(BF16)
    o += D_B
    g = proj(o, D_B)
    gb_ref[...] = (g * _sigmoid(g)).astype(BF16)
    o += D_B
    d_model = ma_ref.shape[1]
    ma_ref[...] = _sigmoid(proj(o, d_model)).astype(BF16)
    o += d_model
    mb_ref[...] = _sigmoid(proj(o, d_model)).astype(BF16)


def _inproj(x2d, seq_len, pos0, norm_w, w_in_bf, q_norm_w, k_norm_w, ones_bd, tm):
    n, d_model = x2d.shape
    d_in = w_in_bf.shape[1]
    half = DH_B // 2
    inv = ROPE_THETA ** (-jnp.arange(half, dtype=F32) / half)
    pos = (pos0 + jnp.arange(seq_len, dtype=jnp.int32)).astype(F32)
    ang = pos[:, None] * inv[None, :]
    cos_h, sin_h = jnp.cos(ang), jnp.sin(ang)
    cos_t = jnp.tile(cos_h, (1, LANES // half))
    sin_t = jnp.tile(jnp.concatenate([-sin_h, sin_h], axis=1), (1, LANES // DH_B))
    if seq_len >= tm:
        assert seq_len % tm == 0
        tab_blocks = seq_len // tm
    else:
        assert tm % seq_len == 0
        cos_t = jnp.tile(cos_t, (tm // seq_len, 1))
        sin_t = jnp.tile(sin_t, (tm // seq_len, 1))
        tab_blocks = 1
    qg = jnp.tile(q_norm_w.reshape(1, 2 * DH_B), (1, H_B))
    kg = jnp.tile(k_norm_w.reshape(1, 2 * DH_B), (1, H_B))

    row = lambda w: pl.BlockSpec((tm, w), lambda i: (i, 0))
    const = lambda a: pl.BlockSpec(a.shape, lambda i: (0,) * a.ndim)
    tab = pl.BlockSpec((tm, LANES), lambda i: (i % tab_blocks, 0))
    nw = norm_w.reshape(1, d_model)
    out_shapes = [
        jax.ShapeDtypeStruct((n, D_SHIFT), F32), jax.ShapeDtypeStruct((n, D_A), BF16),
        jax.ShapeDtypeStruct((n, D_B), BF16),
        jax.ShapeDtypeStruct((n, D_B), F32), jax.ShapeDtypeStruct((n, D_B), BF16),
        jax.ShapeDtypeStruct((n, D_B), F32), jax.ShapeDtypeStruct((n, D_B), BF16),
        jax.ShapeDtypeStruct((n, D_B), BF16),
        jax.ShapeDtypeStruct((n, d_model), BF16), jax.ShapeDtypeStruct((n, d_model), BF16),
    ]
    assert d_in == D_SHIFT + D_A + 4 * D_B + 2 * d_model
    return pl.pallas_call(
        _inproj_kernel,
        grid=(n // tm,),
        in_specs=[row(d_model), const(nw), const(w_in_bf), tab, tab, const(qg), const(kg), const(ones_bd)],
        out_specs=[row(s.shape[1]) for s in out_shapes],
        out_shape=out_shapes,
        compiler_params=pltpu.CompilerParams(dimension_semantics=("parallel",), vmem_limit_bytes=VMEM_LIMIT),
        name="inproj",
    )(x2d, nw, w_in_bf, cos_t, sin_t, qg, kg, ones_bd)


def _block_rows(x, reps, row_block, col_block):
    t = jnp.concatenate([x] * reps, axis=0)
    ri = lax.broadcasted_iota(jnp.int32, t.shape, 0) // row_block
    ci = lax.broadcasted_iota(jnp.int32, t.shape, 1) // col_block
    return jnp.where(ri == ci, t, 0.0)


def _rwkv_chunk(s_ref, g, r, ld, cum, k, v, kk, a, chunk):
    c = chunk
    hg = HEADS_PER_GROUP
    bdv = lambda x: _block_rows(x, hg, c, DH_A).astype(BF16)
    bds = lambda x: _block_rows(x, hg, c, c).astype(BF16)

    e_cum = jnp.exp(cum)
    e_ncum = jnp.exp(-cum)
    cum_end = cum[c - 1:c]
    dec_end = jnp.exp(cum_end - cum)
    b = kk * a
    a_t = -kk * jnp.exp(cum - ld)
    r_t = r * e_cum
    ar = jnp.concatenate([a_t, r_t], axis=0).astype(BF16)
    xb = _dot_nt(ar, bdv(b * e_ncum))
    xk = _dot_nt(ar, bdv(k * e_ncum))
    t_i = lax.broadcasted_iota(jnp.int32, (c, hg * c), 0)
    s_i = lax.broadcasted_iota(jnp.int32, (c, hg * c), 1) % c
    strict = t_i > s_i
    incl = t_i >= s_i
    l_m = jnp.where(strict, xb[:c], 0.0)
    a_ak = jnp.where(strict, xk[:c], 0.0)
    a_rb = jnp.where(incl, xb[c:], 0.0)
    a_rk = jnp.where(incl, xk[c:], 0.0)
    t_m = jnp.where(t_i == s_i, 1.0, 0.0) + l_m
    n_dbl = int(math.log2(c))
    assert 2 ** n_dbl == c
    p = _dot(l_m.astype(BF16), bds(l_m))
    for i in range(1, n_dbl):
        if i < n_dbl - 1:
            tp = _dot(jnp.concatenate([t_m, p], axis=0).astype(BF16), bds(p))
            t_m = t_m + tp[:c]
            p = tp[c:]
        else:
            t_m = t_m + _dot(t_m.astype(BF16), bds(p))
    bd_v = bdv(v)
    s_old = s_ref[g]
    xs = _dot_nt(ar, s_old.astype(BF16))
    rhs = xs[:c] + _dot(a_ak.astype(BF16), bd_v)
    u = _dot(t_m.astype(BF16), bdv(rhs))
    o = xs[c:] + _dot(a_rb.astype(BF16), bdv(u)) + _dot(a_rk.astype(BF16), bd_v)
    uv = jnp.concatenate([u, v], axis=0).astype(BF16)
    bk = jnp.concatenate([b * dec_end, k * dec_end], axis=0).astype(BF16)
    upd = _dot_tn(uv, bk)
    ri = lax.broadcasted_iota(jnp.int32, upd.shape, 0) // DH_A
    ci = lax.broadcasted_iota(jnp.int32, upd.shape, 1) // DH_A
    s_ref[g] = s_old * jnp.exp(cum_end) + jnp.where(ri == ci, upd, 0.0)
    return o


def _rwkv_kernel(ps_ref, ga_ref, s0_ref, prev_ref, mu_ref, w0_ref, wup_ref, a0_ref, aup_ref,
                 kk_ref, ka_ref, rk_ref, gnw_ref, gnb_ref, ones_ref, tri_ref, exp_ref, fold_ref,
                 o_ref, sout_ref,
                 s_ref, prev_scr, r_scr, ld_scr, cum_scr, k_scr, v_scr, kkn_scr, a_scr, o_scr, *, chunk):
    i = pl.program_id(1)
    tt = ps_ref.shape[1]

    @pl.when(i == 0)
    def _():
        prev_scr[...] = prev_ref[0]
        for g in range(N_GROUPS):
            flat = s0_ref[0, g * HEADS_PER_GROUP:(g + 1) * HEADS_PER_GROUP].reshape(GROUP_W, DH_A)
            wide = _dot_sel3(flat, exp_ref[...])
            ri = lax.broadcasted_iota(jnp.int32, wide.shape, 0) // DH_A
            ci = lax.broadcasted_iota(jnp.int32, wide.shape, 1) // DH_A
            s_ref[g] = jnp.where(ri == ci, wide, 0.0)

    p = ps_ref[0]
    rolled = pltpu.roll(p, 1, axis=0)
    row0 = lax.broadcasted_iota(jnp.int32, p.shape, 0) == 0
    shifted = jnp.where(row0, jnp.broadcast_to(prev_scr[...], p.shape), rolled)
    prev_scr[...] = p[tt - 1:tt]
    xm = p + (shifted - p) * mu_ref[...]
    r = xm[:, 0:D_A]
    k = xm[:, D_A:2 * D_A]
    v = xm[:, 2 * D_A:3 * D_A]
    z = xm[:, 3 * D_A:D_SHIFT]
    w = w0_ref[...] + _dot(jnp.tanh(z).astype(BF16), wup_ref[...])
    sp = jnp.maximum(-w, 0.0) + jnp.log(1.0 + jnp.exp(-jnp.abs(w)))
    ld = -jnp.exp(-sp - 0.5)
    a = _sigmoid(a0_ref[...] + _dot(z.astype(BF16), aup_ref[...]))
    ones_bd = ones_ref[...]
    kk = k * kk_ref[...]
    kk = kk / jnp.maximum(jnp.sqrt(_group_sum(kk * kk, ones_bd)), 1e-12)
    k = k * (1.0 + (a - 1.0) * ka_ref[...])
    bonus = _group_sum(r * k * rk_ref[...], ones_bd) * v
    r_scr[...] = r
    ld_scr[...] = ld
    k_scr[...] = k
    v_scr[...] = v
    kkn_scr[...] = kk
    a_scr[...] = a
    tri = tri_ref[...]
    for c0 in range(0, tt, chunk):
        cum_scr[c0:c0 + chunk, :] = _sel3_dot(tri, ld[c0:c0 + chunk])

    for c0 in range(0, tt, chunk):
        rows = slice(c0, c0 + chunk)
        for g in range(N_GROUPS):
            cols = slice(g * GROUP_W, (g + 1) * GROUP_W)
            o_scr[rows, cols] = _rwkv_chunk(
                s_ref, g, r_scr[rows, cols], ld_scr[rows, cols], cum_scr[rows, cols],
                k_scr[rows, cols], v_scr[rows, cols], kkn_scr[rows, cols], a_scr[rows, cols], chunk)

    o = o_scr[...]
    mean = _group_sum(o, ones_bd) * (1.0 / DH_A)
    d = o - mean
    var = _group_sum(d * d, ones_bd) * (1.0 / DH_A)
    on = d * lax.rsqrt(var + GN_EPS) * gnw_ref[...] + gnb_ref[...] + bonus
    o_ref[0] = (on * ga_ref[0].astype(F32)).astype(BF16)

    @pl.when(i == pl.num_programs(1) - 1)
    def _():
        for g in range(N_GROUPS):
            folded = _dot_sel3(s_ref[g], fold_ref[...])
            sout_ref[0, g * HEADS_PER_GROUP:(g + 1) * HEADS_PER_GROUP] = folded.reshape(
                HEADS_PER_GROUP, DH_A, DH_A)


def _rwkv(ps, ga, s0, prev_row, mu, w0, w_up, a0, a_up, k_k, k_a, r_k, gn_w, gn_b, ones_bd, tt, chunk):
    b, t, _ = ps.shape
    assert t % tt == 0 and tt % chunk == 0
    row = lambda a: a.reshape(1, -1).astype(F32)
    wup_pad = jnp.concatenate([w_up, jnp.zeros_like(a_up)], axis=0).astype(BF16)
    aup_pad = jnp.concatenate([jnp.zeros_like(w_up), a_up], axis=0).astype(BF16)
    tri = jnp.tril(jnp.ones((chunk, chunk), F32)).astype(BF16)
    eye = jnp.eye(DH_A, dtype=F32)
    expand = jnp.tile(eye, (1, HEADS_PER_GROUP)).astype(BF16)
    fold = jnp.tile(eye, (HEADS_PER_GROUP, 1)).astype(BF16)
    consts = [row(mu), row(w0), wup_pad, row(a0), aup_pad, row(k_k), row(k_a), row(r_k), row(gn_w),
              row(gn_b), ones_bd, tri, expand, fold]
    const = lambda a: pl.BlockSpec(a.shape, lambda bi, ti: (0,) * a.ndim)
    tile_f32 = pltpu.VMEM((tt, D_A), F32)
    return pl.pallas_call(
        functools.partial(_rwkv_kernel, chunk=chunk),
        grid=(b, t // tt),
        in_specs=[pl.BlockSpec((1, tt, D_SHIFT), lambda bi, ti: (bi, ti, 0)),
                  pl.BlockSpec((1, tt, D_A), lambda bi, ti: (bi, ti, 0)),
                  pl.BlockSpec((1, H_A, DH_A, DH_A), lambda bi, ti: (bi, 0, 0, 0)),
                  pl.BlockSpec((1, 1, D_SHIFT), lambda bi, ti: (bi, 0, 0))]
                 + [const(a) for a in consts],
        out_specs=[pl.BlockSpec((1, tt, D_A), lambda bi, ti: (bi, ti, 0)),
                   pl.BlockSpec((1, H_A, DH_A, DH_A), lambda bi, ti: (bi, 0, 0, 0))],
        out_shape=[jax.ShapeDtypeStruct((b, t, D_A), BF16),
                   jax.ShapeDtypeStruct((b, H_A, DH_A, DH_A), F32)],
        scratch_shapes=[pltpu.VMEM((N_GROUPS, GROUP_W, GROUP_W), F32), pltpu.VMEM((1, D_SHIFT), F32)]
                       + [tile_f32] * 8,
        compiler_params=pltpu.CompilerParams(dimension_semantics=("parallel", "arbitrary"),
                                             vmem_limit_bytes=VMEM_LIMIT),
        name="rwkv",
    )(ps, ga, s0, prev_row.reshape(b, 1, D_SHIFT), *consts)


def _lambda_full(lam_ref, lam_init):
    lv = lam_ref[...]
    s1 = jnp.sum(lv[0:1] * lv[1:2], axis=-1, keepdims=True)
    s2 = jnp.sum(lv[2:3] * lv[3:4], axis=-1, keepdims=True)
    return jnp.exp(s1) - jnp.exp(s2) + lam_init


def _subln_gate(o, subln_w, gate, lam_init):
    outs = []
    for c in range(0, o.shape[1], 2 * DH_B):
        oc = o[:, c:c + 2 * DH_B]
        ms = jnp.mean(oc * oc, axis=-1, keepdims=True)
        outs.append(oc * lax.rsqrt(ms + 1e-5) * subln_w)
    y = outs[0] if len(outs) == 1 else jnp.concatenate(outs, axis=1)
    return (y * (1.0 - lam_init) * gate.astype(F32)).astype(BF16)


def _pattn_kernel(q_ref, k_ref, v_ref, gb_ref, sub_ref, lam_ref, o_ref,
                  m_scr, l_scr, acc_scr, *, tile, lam_init):
    qi = pl.program_id(2)
    q = q_ref[0]
    lane = lax.broadcasted_iota(jnp.int32, (1, 2 * DH_B), 1)
    qs = [jnp.where(lane < DH_B, q, jnp.zeros_like(q)), jnp.where(lane >= DH_B, q, jnp.zeros_like(q))]
    m_scr[...] = jnp.full(m_scr.shape, -jnp.inf, F32)
    l_scr[...] = jnp.zeros(l_scr.shape, F32)
    acc_scr[...] = jnp.zeros(acc_scr.shape, F32)

    def step(j, masked):
        kt = k_ref[0, pl.ds(j * tile, tile), :]
        vt = v_ref[0, pl.ds(j * tile, tile), :]
        for c in range(2):
            s = _dot_nt(qs[c], kt)
            if masked:
                qpos = lax.broadcasted_iota(jnp.int32, s.shape, 0)
                kpos = lax.broadcasted_iota(jnp.int32, s.shape, 1)
                s = jnp.where(kpos <= qpos, s, -jnp.inf)
            m_old = m_scr[c]
            m_new = jnp.maximum(m_old, jnp.max(s, axis=-1, keepdims=True))
            alpha = jnp.exp(m_old - m_new)
            p = jnp.exp(s - m_new)
            l_scr[c] = alpha * l_scr[c] + jnp.sum(p, axis=-1, keepdims=True)
            acc_scr[c] = alpha * acc_scr[c] + _dot(p.astype(BF16), vt)
            m_scr[c] = m_new

    def body(j, carry):
        step(j, False)
        return carry

    lax.fori_loop(0, qi, body, 0)
    step(qi, True)

    lam = _lambda_full(lam_ref, lam_init)
    o = acc_scr[0] / l_scr[0] - lam * (acc_scr[1] / l_scr[1])
    o_ref[0] = _subln_gate(o, sub_ref[...], gb_ref[0], lam_init)


def _prompt_attn(q, k, v, gb, subln_w, lam_vecs, lam_init, tile):
    b, t, _ = q.shape
    hw = 2 * DH_B
    assert t % tile == 0
    qspec = pl.BlockSpec((1, tile, hw), lambda bi, h, qi: (bi, qi, h))
    kvspec = pl.BlockSpec((1, t, hw), lambda bi, h, qi: (bi, 0, h))
    const = lambda a: pl.BlockSpec(a.shape, lambda bi, h, qi: (0,) * a.ndim)
    sub = subln_w.reshape(1, hw)
    return pl.pallas_call(
        functools.partial(_pattn_kernel, tile=tile, lam_init=lam_init),
        grid=(b, H_B, t // tile),
        in_specs=[qspec, kvspec, kvspec, qspec, const(sub), const(lam_vecs)],
        out_specs=qspec,
        out_shape=jax.ShapeDtypeStruct((b, t, D_B), BF16),
        scratch_shapes=[pltpu.VMEM((2, tile, 1), F32), pltpu.VMEM((2, tile, 1), F32),
                        pltpu.VMEM((2, tile, hw), F32)],
        compiler_params=pltpu.CompilerParams(dimension_semantics=("parallel", "parallel", "arbitrary"),
                                             vmem_limit_bytes=VMEM_LIMIT),
        name="prompt_attn",
    )(q, k, v, gb, sub, lam_vecs)


def _sattn_kernel(pt_ref, q_ref, kn_ref, vn_ref, gb_ref, sub_ref, lam_ref, *rest,
                  pages_per_step, lam_init):
    kp_refs = rest[:pages_per_step]
    vp_refs = rest[pages_per_step:2 * pages_per_step]
    o_ref = rest[2 * pages_per_step]
    m_scr, l_scr, acc_scr, qx_scr = rest[2 * pages_per_step + 1:]
    j = pl.program_id(1)
    tq = q_ref.shape[1]
    n_rows = 2 * H_B * tq

    @pl.when(j == 0)
    def _():
        qx_scr[...] = _block_rows(q_ref[0].astype(F32), 2 * H_B, tq, DH_B).astype(BF16)
        m_scr[...] = jnp.full(m_scr.shape, -jnp.inf, F32)
        l_scr[...] = jnp.zeros(l_scr.shape, F32)
        acc_scr[...] = jnp.zeros(acc_scr.shape, F32)

    qx = qx_scr[...]
    hw = 2 * DH_B
    hrows = 2 * tq

    def accumulate(s, v_heads):
        m_old = m_scr[...]
        m_new = jnp.maximum(m_old, jnp.max(s, axis=-1, keepdims=True))
        alpha = jnp.exp(m_old - m_new)
        p = jnp.exp(s - m_new)
        l_scr[...] = alpha * l_scr[...] + jnp.sum(p, axis=-1, keepdims=True)
        pb = p.astype(BF16)
        for h in range(H_B):
            rows = slice(h * hrows, (h + 1) * hrows)
            acc_scr[rows, :] = alpha[rows] * acc_scr[rows, :] + _dot(pb[rows], v_heads[h])
        m_scr[...] = m_new

    s = jnp.concatenate([_dot(qx, kp[...].astype(BF16)) for kp in kp_refs], axis=1)
    accumulate(s, [jnp.concatenate([vp[:, h, :] for vp in vp_refs], axis=0).astype(BF16)
                   for h in range(H_B)])

    @pl.when(j == pl.num_programs(1) - 1)
    def _():
        pad = jnp.zeros((LANES - tq, D_B), F32)
        kn = jnp.concatenate([kn_ref[0], pad], axis=0).astype(BF16)
        vn = jnp.concatenate([vn_ref[0], pad], axis=0).astype(BF16)
        sn = _dot_nt(qx, kn)
        qpos = lax.broadcasted_iota(jnp.int32, sn.shape, 0) % tq
        kpos = lax.broadcasted_iota(jnp.int32, sn.shape, 1)
        sn = jnp.where(kpos <= qpos, sn, -jnp.inf)
        accumulate(sn, [vn[:, h * hw:(h + 1) * hw] for h in range(H_B)])
        lam = _lambda_full(lam_ref, lam_init)
        accn = acc_scr[...] / l_scr[...]
        o = jnp.concatenate([accn[h * hrows:h * hrows + tq] - lam * accn[h * hrows + tq:(h + 1) * hrows]
                             for h in range(H_B)], axis=1)
        o_ref[0] = _subln_gate(o, sub_ref[...], gb_ref[0], lam_init)


def _sample_attn(q, kn, vn, gb, cache_k, cache_v, page_table, subln_w, lam_vecs, lam_init, pages_per_step):
    b, tq, _ = q.shape
    n_pool, page = cache_k.shape[0], cache_k.shape[1]
    n_pages = page_table.shape[1]
    assert n_pages % pages_per_step == 0
    ck = jnp.transpose(cache_k, (0, 2, 3, 4, 1)).reshape(n_pool, D_B, page)
    pt = page_table.reshape(-1)
    n_rows = 2 * H_B * tq
    hw = 2 * DH_B
    sub = subln_w.reshape(1, hw)

    seq = lambda w: pl.BlockSpec((1, tq, w), lambda bi, j, pt: (bi, 0, 0))
    const = lambda a: pl.BlockSpec(a.shape, lambda bi, j, pt: (0,) * a.ndim)

    def page_index(i):
        return lambda bi, j, pt: (pt[bi * n_pages + j * pages_per_step + i], 0, 0)

    k_pages = [pl.BlockSpec((None, D_B, page), page_index(i)) for i in range(pages_per_step)]
    v_pages = [pl.BlockSpec((None, page, H_B, hw), lambda bi, j, pt, f=page_index(i): f(bi, j, pt) + (0,))
               for i in range(pages_per_step)]
    grid_spec = pltpu.PrefetchScalarGridSpec(
        num_scalar_prefetch=1,
        grid=(b, n_pages // pages_per_step),
        in_specs=[seq(D_B), seq(D_B), seq(D_B), seq(D_B), const(sub), const(lam_vecs)] + k_pages + v_pages,
        out_specs=seq(D_B),
        scratch_shapes=[pltpu.VMEM((n_rows, 1), F32), pltpu.VMEM((n_rows, 1), F32),
                        pltpu.VMEM((n_rows, hw), F32), pltpu.VMEM((n_rows, D_B), BF16)],
    )
    return pl.pallas_call(
        functools.partial(_sattn_kernel, pages_per_step=pages_per_step, lam_init=lam_init),
        grid_spec=grid_spec,
        out_shape=jax.ShapeDtypeStruct((b, tq, D_B), BF16),
        compiler_params=pltpu.CompilerParams(dimension_semantics=("parallel", "arbitrary"),
                                             vmem_limit_bytes=VMEM_LIMIT),
        name="sample_attn",
    )(pt, q, kn, vn, gb, sub, lam_vecs, *([ck] * pages_per_step), *([cache_v] * pages_per_step))


def _tail_kernel(x_ref, oa_ref, ob_ref, ma_ref, mb_ref, wa_ref, wb_ref, wo_ref, y_ref):
    ya = _dot(oa_ref[...], wa_ref[...])
    yb = _dot(ob_ref[...], wb_ref[...])
    merged = ma_ref[...].astype(F32) * ya + mb_ref[...].astype(F32) * yb
    y_ref[...] = x_ref[...] + _dot(merged.astype(BF16), wo_ref[...])


def _tail(x2d, oa, ob, ma, mb, wa, wb, wo, tm):
    n, d_model = x2d.shape
    row = lambda w: pl.BlockSpec((tm, w), lambda i: (i, 0))
    const = lambda a: pl.BlockSpec(a.shape, lambda i: (0, 0))
    return pl.pallas_call(
        _tail_kernel,
        grid=(n // tm,),
        in_specs=[row(d_model), row(D_A), row(D_B), row(d_model), row(d_model), const(wa), const(wb), const(wo)],
        out_specs=row(d_model),
        out_shape=jax.ShapeDtypeStruct((n, d_model), F32),
        compiler_params=pltpu.CompilerParams(dimension_semantics=("parallel",), vmem_limit_bytes=VMEM_LIMIT),
        name="tail",
    )(x2d, oa, ob, ma, mb, wa, wb, wo)


def _pick(n, target):
    t = min(n, target)
    while n % t:
        t //= 2
    return t


def _layer(x, pos0, s0, prev_row, lam_init, lw, ones_bd, attend):
    (norm_w, w_in_bf, mu, w0, w_up, a0, a_up, k_k, k_a, r_k, gn_w, gn_b, q_norm_w, k_norm_w,
     lam_vecs, subln_w, wa_bf, wb_bf, wo_bf) = lw
    b, t, d_model = x.shape
    n = b * t
    x2d = x.reshape(n, d_model)
    tm = _pick(n, 256)
    ps, ga, q, kf, kb, vf, vb, gb, ma, mb = _inproj(x2d, t, pos0, norm_w, w_in_bf, q_norm_w, k_norm_w,
                                                    ones_bd, tm)
    chunk = min(t, 64)
    tt = _pick(t, 256)
    oa, s_final = _rwkv(ps.reshape(b, t, D_SHIFT), ga.reshape(b, t, D_A), s0, prev_row, mu, w0, w_up,
                        a0, a_up, k_k, k_a, r_k, gn_w, gn_b, ones_bd, tt, chunk)
    r3 = lambda a: a.reshape(b, t, D_B)
    ob = attend(r3(q), r3(kb), r3(vb), r3(kf), r3(vf), r3(gb), subln_w, lam_vecs)
    y = _tail(x2d, oa.reshape(n, D_A), ob.reshape(n, D_B), ma, mb, wa_bf, wb_bf, wo_bf, _pick(n, 512))
    shift_row = ps.reshape(b, t, D_SHIFT)[:, -1]
    return (y.reshape(b, t, d_model), kf.reshape(b, t, H_B, 2, DH_B), vf.reshape(b, t, H_B, 2 * DH_B),
            s_final, shift_row)


def kernel(x_prompt, x_sample, cache_k, cache_v, state_wkv, state_shift, page_table, norm_w, w_in,
           mu_shift, w0, w_up, a0, a_up, k_k, k_a, r_k, gn_w, gn_b, q_norm_w, k_norm_w, lambda_q1,
           lambda_k1, lambda_q2, lambda_k2, subln_w, w_branch_a, w_branch_b, w_out):
    depth = norm_w.shape[0]
    b_p, t_p = x_prompt.shape[0], x_prompt.shape[1]
    page = cache_k.shape[2]
    past = page_table.shape[1] * page
    gi = lax.broadcasted_iota(jnp.int32, (GROUP_W, GROUP_W), 0) // DH_A
    gj = lax.broadcasted_iota(jnp.int32, (GROUP_W, GROUP_W), 1) // DH_A
    ones_bd = (gi == gj).astype(BF16)
    y_p, y_s = x_prompt, x_sample
    outs = [[] for _ in range(8)]
    for l in range(depth):
        lam_init = 0.8 - 0.6 * math.exp(-0.3 * l)
        lam_vecs = jnp.stack([lambda_q1[l], lambda_k1[l], lambda_q2[l], lambda_k2[l]]).astype(F32)
        lw = (norm_w[l], w_in[l].astype(BF16), mu_shift[l], w0[l], w_up[l], a0[l], a_up[l], k_k[l], k_a[l],
              r_k[l], gn_w[l], gn_b[l], q_norm_w[l], k_norm_w[l], lam_vecs, subln_w[l],
              w_branch_a[l].astype(BF16), w_branch_b[l].astype(BF16), w_out[l].astype(BF16))
        zero_state = jnp.zeros((b_p, H_A, DH_A, DH_A), F32)
        zero_row = jnp.zeros((b_p, D_SHIFT), F32)

        def attend_p(q, kb, vb, kf, vf, gb, sub, lv, lam_init=lam_init):
            return _prompt_attn(q, kb, vb, gb, sub, lv, lam_init, _pick(q.shape[1], 512))

        def attend_s(q, kb, vb, kf, vf, gb, sub, lv, lam_init=lam_init, l=l):
            return _sample_attn(q, kf, vf, gb, cache_k[l], cache_v[l], page_table, sub, lv, lam_init,
                                _pick(page_table.shape[1], 16))

        y_p, k_r, v_r, s_r, h_r = _layer(y_p, 0, zero_state, zero_row, lam_init, lw, ones_bd, attend_p)
        for dst, val in zip(outs[:4], (k_r, v_r, s_r, h_r)):
            dst.append(val)
        y_s, k_r, v_r, s_r, h_r = _layer(y_s, past, state_wkv[l], state_shift[l], lam_init, lw, ones_bd,
                                         attend_s)
        for dst, val in zip(outs[4:], (k_r, v_r, s_r, h_r)):
            dst.append(val)
    return (y_p, y_s) + tuple(jnp.stack(o) for o in outs)
```
